```python
import jax, jax.numpy as jnp
from jax import lax
import numpy as np

D_MODEL = 1024
BATCH = 2
SEQ = 8192
DEPTH = 1

CHUNK = 64
Q_BLOCK = 128
ATTN_HEADS = 8
HEAD_DIM = 64
ATTN_WIDTH = ATTN_HEADS * HEAD_DIM
CONV_WIDTH = D_MODEL - ATTN_WIDTH
CONV_GROUPS = 8
CONV_KERNEL = 31
IDX_HEADS = 8
IDX_DIM = 64
TOPK_MAX = 256
ROT_DIM = HEAD_DIM // 4
ROPE_THETA = 500000.0
D_FF = 4 * D_MODEL
N_MOD = 6
EPS = 1e-6
IN_SIZES = (ATTN_WIDTH, ATTN_WIDTH, ATTN_WIDTH, IDX_HEADS * IDX_DIM, IDX_DIM, IDX_HEADS, CONV_WIDTH, CONV_WIDTH)
IN_COLS = 3 * ATTN_WIDTH + IDX_HEADS * IDX_DIM + IDX_DIM + IDX_HEADS + 2 * CONV_WIDTH

kernel_name = "hybrid_dsa_conformer_adaln_layer"


def _split_points():
    pts, acc = [], 0
    for s in IN_SIZES[:-1]:
        acc += s
        pts.append(acc)
    return pts


def rms_norm(x, g):
    xf = x.astype(jnp.float32)
    y = xf * lax.rsqrt(jnp.mean(xf * xf, axis=-1, keepdims=True) + EPS)
    return (y * g.astype(jnp.float32)).astype(x.dtype)


def layer_norm(x, g, b):
    xf = x.astype(jnp.float32)
    mu = jnp.mean(xf, axis=-1, keepdims=True)
    var = jnp.mean(jnp.square(xf - mu), axis=-1, keepdims=True)
    y = (xf - mu) * lax.rsqrt(var + EPS)
    return (y * g.astype(jnp.float32) + b.astype(jnp.float32)).astype(x.dtype)


def rope_tables(seq_len, dtype):
    pos = jnp.arange(seq_len, dtype=jnp.float32)
    inv_freq = ROPE_THETA ** (-jnp.arange(0, ROT_DIM, 2, dtype=jnp.float32) / ROT_DIM)
    ang = pos[:, None] * inv_freq[None, :]
    return jnp.cos(ang).astype(dtype), jnp.sin(ang).astype(dtype)


def partial_rope(x, cos, sin):
    half = ROT_DIM // 2
    x1 = x[..., :half]
    x2 = x[..., half:ROT_DIM]
    c = cos[None, :, None, :]
    s = sin[None, :, None, :]
    return jnp.concatenate([x1 * c - x2 * s, x2 * c + x1 * s, x[..., ROT_DIM:]], axis=-1)


def dsa_sparse_attention(q, k, v, iq, ik, iw):
    B, S, H, dh = q.shape
    n_blk = S // Q_BLOCK
    top_k = min(TOPK_MAX, S // 4)
    key_chunk = jnp.arange(S) // CHUNK
    ik_f = ik.astype(jnp.float32)

    def block(i):
        start = i * Q_BLOCK
        q_b = lax.dynamic_slice_in_dim(q, start, Q_BLOCK, axis=1)
        iq_b = lax.dynamic_slice_in_dim(iq, start, Q_BLOCK, axis=1).astype(jnp.float32)
        iw_b = lax.dynamic_slice_in_dim(iw, start, Q_BLOCK, axis=1).astype(jnp.float32)
        q_chunk = (start + jnp.arange(Q_BLOCK)) // CHUNK
        admissible = key_chunk[None, :] <= q_chunk[:, None]
        rel = jax.nn.relu(jnp.einsum('bthd,bsd->bths', iq_b, ik_f) * (IDX_DIM ** -0.5))
        score = jnp.einsum('bths,bth->bts', rel, iw_b)
        score = jnp.where(admissible[None], score, -jnp.inf)
        _, idx = lax.top_k(score, top_k)
        valid = key_chunk[idx] <= q_chunk[None, :, None]
        k_sel = jax.vmap(lambda kb, ib: kb[ib])(k, idx)
        v_sel = jax.vmap(lambda vb, ib: vb[ib])(v, idx)
        logits = jnp.einsum('bthd,btkhd->bthk', q_b, k_sel).astype(jnp.float32) * (dh ** -0.5)
        logits = jnp.where(valid[:, :, None, :], logits, -jnp.inf)
        p = jax.nn.softmax(logits, axis=-1).astype(v.dtype)
        return jnp.einsum('bthk,btkhd->bthd', p, v_sel)

    out = lax.map(block, jnp.arange(n_blk))
    return jnp.transpose(out, (1, 0, 2, 3, 4)).reshape(B, S, H * dh)


def conformer_conv(a, g, w_dw, b_dw, ln_g, ln_b):
    u = a * jax.nn.sigmoid(g)
    u = jnp.pad(u, ((0, 0), (CONV_KERNEL - 1, 0), (0, 0)))
    y = lax.conv_general_dilated(u, w_dw[:, None, :], window_strides=(1,), padding='VALID',
                                 dimension_numbers=('NWC', 'WIO', 'NWC'),
                                 feature_group_count=CONV_WIDTH) + b_dw
    y = layer_norm(y, ln_g, ln_b)
    return jax.nn.silu(y)


def setup_inputs(seed: int = 0) -> dict:
    key = jax.random.key(seed)
    ks = jax.random.split(key, 20)

    def nrm(k, shape, scale):
        return jax.random.normal(k, shape, jnp.float32) * scale

    L = DEPTH
    return {
        "x": nrm(ks[0], (BATCH, SEQ, D_MODEL), 1.0),
        "c": nrm(ks[1], (BATCH, D_MODEL), 1.0),
        "w_ada": nrm(ks[2], (L, D_MODEL, N_MOD * D_MODEL), 0.5 * D_MODEL ** -0.5),
        "b_ada": nrm(ks[3], (L, N_MOD * D_MODEL), 0.02),
        "g_norm1": 1.0 + nrm(ks[4], (L, D_MODEL), 0.02),
        "w_in": nrm(ks[5], (L, D_MODEL, IN_COLS), D_MODEL ** -0.5),
        "g_q": 1.0 + nrm(ks[6], (L, HEAD_DIM), 0.02),
        "g_k": 1.0 + nrm(ks[7], (L, HEAD_DIM), 0.02),
        "w_dw": nrm(ks[8], (L, CONV_KERNEL, CONV_WIDTH), CONV_KERNEL ** -0.5),
        "b_dw": nrm(ks[9], (L, CONV_WIDTH), 0.02),
        "g_conv_ln": 1.0 + nrm(ks[10], (L, CONV_WIDTH), 0.02),
        "b_conv_ln": nrm(ks[11], (L, CONV_WIDTH), 0.02),
        "g_out_attn": 1.0 + nrm(ks[12], (L, ATTN_WIDTH), 0.02),
        "g_out_conv": 1.0 + nrm(ks[13], (L, CONV_WIDTH), 0.02),
        "w_out": nrm(ks[14], (L, D_MODEL, D_MODEL), D_MODEL ** -0.5),
        "g_norm2": 1.0 + nrm(ks[15], (L, D_MODEL), 0.02),
        "w_ff1": nrm(ks[16], (L, D_MODEL, D_FF), D_MODEL ** -0.5),
        "w_ff2": nrm(ks[17], (L, D_FF, D_MODEL), D_FF ** -0.5),
    }


def reference(x, c, w_ada, b_ada, g_norm1, w_in, g_q, g_k, w_dw, b_dw, g_conv_ln, b_conv_ln,
              g_out_attn, g_out_conv, w_out, g_norm2, w_ff1, w_ff2):
    B, S, _ = x.shape
    cos, sin = rope_tables(S, x.dtype)
    c_act = jax.nn.silu(c)
    split_pts = _split_points()
    for l in range(DEPTH):
        mod = (c_act @ w_ada[l] + b_ada[l])[:, None, :]
        sh1, sc1, gt1, sh2, sc2, gt2 = jnp.split(mod, N_MOD, axis=-1)

        h = rms_norm(x, g_norm1[l]) * (1.0 + sc1) + sh1
        u = h @ w_in[l]
        q, k, v, iq, ik, iw, ca, cg = jnp.split(u, split_pts, axis=-1)
        q = partial_rope(rms_norm(q.reshape(B, S, ATTN_HEADS, HEAD_DIM), g_q[l]), cos, sin)
        k = partial_rope(rms_norm(k.reshape(B, S, ATTN_HEADS, HEAD_DIM), g_k[l]), cos, sin)
        v = v.reshape(B, S, ATTN_HEADS, HEAD_DIM)
        iq = partial_rope(iq.reshape(B, S, IDX_HEADS, IDX_DIM), cos, sin)
        ik = partial_rope(ik.reshape(B, S, 1, IDX_DIM), cos, sin)[:, :, 0, :]
        iw = iw * (IDX_HEADS ** -0.5)
        attn = dsa_sparse_attention(q, k, v, iq, ik, iw)
        conv = conformer_conv(ca, cg, w_dw[l], b_dw[l], g_conv_ln[l], b_conv_ln[l])
        mixed = jnp.concatenate([rms_norm(attn, g_out_attn[l]), rms_norm(conv, g_out_conv[l])], axis=-1)
        x = x + gt1 * (mixed @ w_out[l])

        h2 = rms_norm(x, g_norm2[l]) * (1.0 + sc2) + sh2
        f = jnp.square(jax.nn.relu(h2 @ w_ff1[l])) @ w_ff2[l]
        x = x + gt2 * f
    return x
```

```python
import functools

import jax
import jax.numpy as jnp
import numpy as np
from jax import lax
from jax.experimental import pallas as pl
from jax.experimental.pallas import tpu as pltpu

F32 = jnp.float32
BF16 = jnp.bfloat16

CHUNK = 64
ATTN_HEADS = 8
HEAD_DIM = 64
ATTN_WIDTH = ATTN_HEADS * HEAD_DIM
IDX_HEADS = 8
IDX_DIM = 64
TOPK_MAX = 256
ROT_DIM = HEAD_DIM // 4
ROPE_THETA = 500000.0
CONV_KERNEL = 31
N_MOD = 6
EPS = 1e-6

LANES = 128
SUBLANES = 8
VMEM_LIMIT_BYTES = 56 * 1024 * 1024
TQ = 256
TKS = 256
TKA = 512
assert TKA % TQ == 0 and TKA == 2 * TKS
TM = 512
MOD_COLS = 1024
TC = 256
CONV_HALO = 32
CONV_ROWS = 32
NEG_BIG = -1e30
FLT_MAX = 3.4028234663852886e38
INT_MIN = -2147483648
BISECT_CHECK_BIT = 28
SORT_GROUP = 16


def _batcher_pairs(lo, hi):
    def merge(lo, hi, r):
        step = 2 * r
        if step < hi - lo:
            yield from merge(lo, hi, step)
            yield from merge(lo + r, hi, step)
            for i in range(lo + r, hi - r, step):
                yield (i, i + r)
        else:
            yield (lo, lo + r)

    if hi - lo >= 1:
        mid = lo + (hi - lo) // 2
        yield from _batcher_pairs(lo, mid)
        yield from _batcher_pairs(mid + 1, hi)
        yield from merge(lo, hi, 1)


_SORT_PAIRS = tuple(_batcher_pairs(0, SORT_GROUP - 1))
SORT_STRIDE = (SORT_GROUP + 1) * SUBLANES
LOG2E = 1.4426950408889634
MAX_FIXED_SHIFT = 50.0

_NT = (((1,), (1,)), ((), ()))


def _resident(shape):
    nd = len(shape)
    return pl.BlockSpec(shape, lambda *_: (0,) * nd, pipeline_mode=pl.Buffered(1))


def _mod_kernel(c_ref, w_ref, b_ref, o_ref):
    c = c_ref[...]
    c_act = c * jax.nn.sigmoid(c)
    o_ref[...] = jnp.dot(c_act, w_ref[...], preferred_element_type=F32,
                         precision=lax.Precision.HIGHEST) + b_ref[...]


def _mod_call(c_pad, w_ada, b_ada):
    rows, d = c_pad.shape
    n = w_ada.shape[1]
    tn = MOD_COLS
    return pl.pallas_call(
        _mod_kernel,
        out_shape=jax.ShapeDtypeStruct((rows, n), F32),
        grid=(n // tn,),
        in_specs=[pl.BlockSpec((rows, d), lambda j: (0, 0)),
                  pl.BlockSpec((d, tn), lambda j: (0, j)),
                  pl.BlockSpec((1, tn), lambda j: (0, j))],
        out_specs=pl.BlockSpec((rows, tn), lambda j: (0, j)),
        compiler_params=pltpu.CompilerParams(dimension_semantics=("arbitrary",),
                                             vmem_limit_bytes=VMEM_LIMIT_BYTES),
        name="adaln_mod",
    )(c_pad, w_ada, b_ada)


def _rope_block(y, cos_t, sina_t, sinb_t):
    up = pltpu.roll(y, LANES - ROT_DIM // 2, 1)
    dn = pltpu.roll(y, ROT_DIM // 2, 1)
    return y * cos_t + up * sina_t + dn * sinb_t


def _head_mean_square(u, gmat):
    return jnp.dot((u * u).astype(BF16), gmat, preferred_element_type=F32)


def _inproj_kernel(x_ref, g1_ref, sc1_ref, sh1_ref, wq_ref, wk_ref, wvt_ref, wiq_ref, wik_ref, wiwt_ref,
                   wca_ref, wcg_ref, gq_ref, gk_ref, gmat_ref, cos_ref, sina_ref, sinb_ref,
                   q_ref, k_ref, vt_ref, iq_ref, ik_ref, iwt_ref, glu_ref):
    x = x_ref[0]
    ms = jnp.mean(x * x, axis=-1, keepdims=True)
    h = (x * lax.rsqrt(ms + EPS) * g1_ref[...]) * (1.0 + sc1_ref[0]) + sh1_ref[0]
    hb = h.astype(BF16)

    cos_t = cos_ref[...]
    sina_t = sina_ref[...]
    sinb_t = sinb_ref[...]
    gmat = gmat_ref[...]
    n_blk = ATTN_WIDTH // LANES

    def normed_roped(w_ref, g_ref, scale, out_ref):
        u = jnp.dot(hb, w_ref[...], preferred_element_type=F32)
        y = u * lax.rsqrt(_head_mean_square(u, gmat) + EPS) * g_ref[...]
        for cb in range(n_blk):
            blk = _rope_block(y[:, cb * LANES:(cb + 1) * LANES], cos_t, sina_t, sinb_t)
            out_ref[0, :, cb * LANES:(cb + 1) * LANES] = (blk * scale).astype(out_ref.dtype)

    normed_roped(wq_ref, gq_ref, LOG2E * HEAD_DIM ** -0.5, q_ref)
    normed_roped(wk_ref, gk_ref, 1.0, k_ref)

    vt_ref[0, 0] = lax.dot_general(wvt_ref[...], hb, _NT, preferred_element_type=F32).astype(vt_ref.dtype)

    uiq = jnp.dot(hb, wiq_ref[...], preferred_element_type=F32)
    for cb in range(n_blk):
        blk = _rope_block(uiq[:, cb * LANES:(cb + 1) * LANES], cos_t, sina_t, sinb_t)
        iq_ref[0, :, cb * LANES:(cb + 1) * LANES] = (blk * (IDX_DIM ** -0.5)).astype(iq_ref.dtype)

    uik = jnp.dot(hb, wik_ref[...], preferred_element_type=F32)
    ik_ref[0] = _rope_block(uik, cos_t, sina_t, sinb_t).astype(ik_ref.dtype)

    iwt = lax.dot_general(wiwt_ref[...], hb, _NT, preferred_element_type=F32)
    iwt_ref[0] = iwt * (IDX_HEADS ** -0.5)

    ca = jnp.dot(hb, wca_ref[...], preferred_element_type=F32)
    cg = jnp.dot(hb, wcg_ref[...], preferred_element_type=F32)
    glu_ref[0] = ca * jax.nn.sigmoid(cg)


def _inproj_call(x, g1, sc1, sh1, wq, wk, wvt, wiq, wik, wiwt, wca, wcg, gq, gk, gmat, cos_t, sina_t, sinb_t):
    b, s, d = x.shape
    nt = s // TM
    row = lambda bi, i: (bi, 0, 0)
    tile = lambda bi, i: (bi, i, 0)
    in_specs = [
        pl.BlockSpec((1, TM, d), tile),
        _resident(g1.shape),
        pl.BlockSpec((1, 1, d), row),
        pl.BlockSpec((1, 1, d), row),
        _resident(wq.shape), _resident(wk.shape), _resident(wvt.shape), _resident(wiq.shape),
        _resident(wik.shape), _resident(wiwt.shape), _resident(wca.shape), _resident(wcg.shape),
        _resident(gq.shape), _resident(gk.shape), _resident(gmat.shape),
        pl.BlockSpec((TM, LANES), lambda bi, i: (i, 0)),
        pl.BlockSpec((TM, LANES), lambda bi, i: (i, 0)),
        pl.BlockSpec((TM, LANES), lambda bi, i: (i, 0)),
    ]
    out_shape = [
        jax.ShapeDtypeStruct((b, s, ATTN_WIDTH), BF16),
        jax.ShapeDtypeStruct((b, s, ATTN_WIDTH), BF16),
        jax.ShapeDtypeStruct((b, nt, ATTN_WIDTH, TM), BF16),
        jax.ShapeDtypeStruct((b, s, ATTN_WIDTH), BF16),
        jax.ShapeDtypeStruct((b, s, LANES), BF16),
        jax.ShapeDtypeStruct((b, IDX_HEADS, s), F32),
        jax.ShapeDtypeStruct((b, s, ATTN_WIDTH), F32),
    ]
    out_specs = [
        pl.BlockSpec((1, TM, ATTN_WIDTH), tile),
        pl.BlockSpec((1, TM, ATTN_WIDTH), tile),
        pl.BlockSpec((1, 1, ATTN_WIDTH, TM), lambda bi, i: (bi, i, 0, 0)),
        pl.BlockSpec((1, TM, ATTN_WIDTH), tile),
        pl.BlockSpec((1, TM, LANES), tile),
        pl.BlockSpec((1, IDX_HEADS, TM), lambda bi, i: (bi, 0, i)),
        pl.BlockSpec((1, TM, ATTN_WIDTH), tile),
    ]
    return pl.pallas_call(
        _inproj_kernel,
        out_shape=out_shape,
        grid=(b, nt),
        in_specs=in_specs,
        out_specs=out_specs,
        compiler_params=pltpu.CompilerParams(dimension_semantics=("arbitrary", "arbitrary"),
                                             vmem_limit_bytes=VMEM_LIMIT_BYTES),
        name="inproj",
    )(x, g1, sc1, sh1, wq, wk, wvt, wiq, wik, wiwt, wca, wcg, gq, gk, gmat, cos_t, sina_t, sinb_t)


def _conv_kernel(cur_ref, halo_ref, w_ref, b_ref, lng_ref, lnb_ref, gout_ref, o_ref, buf_ref, y_ref):
    i = pl.program_id(1)
    halo = halo_ref[0]
    halo = jnp.where(i == 0, jnp.zeros_like(halo), halo)
    n_blk = halo.shape[1] // LANES
    for cb in range(n_blk):
        buf_ref[cb, 0:CONV_HALO, :] = halo[:, cb * LANES:(cb + 1) * LANES]
        buf_ref[cb, CONV_HALO:, :] = cur_ref[0, :, cb * LANES:(cb + 1) * LANES]
    first = CONV_HALO - (CONV_KERNEL - 1)

    def row_chunk(r, carry):
        base = pl.multiple_of(r * CONV_ROWS, CONV_ROWS)
        for cb in range(n_blk):
            lanes = slice(cb * LANES, (cb + 1) * LANES)
            acc = jnp.broadcast_to(b_ref[:, lanes], (CONV_ROWS, LANES))
            for j in range(CONV_KERNEL):
                acc = acc + w_ref[j:j + 1, lanes] * buf_ref[cb, pl.ds(base + first + j, CONV_ROWS), :]
            y_ref[pl.ds(base, CONV_ROWS), lanes] = acc
        return carry

    lax.fori_loop(0, TC // CONV_ROWS, row_chunk, 0)
    acc = y_ref[...]
    mu = jnp.mean(acc, axis=-1, keepdims=True)
    cen = acc - mu
    var = jnp.mean(cen * cen, axis=-1, keepdims=True)
    y = cen * lax.rsqrt(var + EPS) * lng_ref[...] + lnb_ref[...]
    y = y * jax.nn.sigmoid(y)
    ms = jnp.mean(y * y, axis=-1, keepdims=True)
    o_ref[0] = (y * lax.rsqrt(ms + EPS) * gout_ref[...]).astype(o_ref.dtype)


def _conv_call(glu, w_dw, b_dw, ln_g, ln_b, g_out):
    b, s, cw = glu.shape
    per = TC // CONV_HALO
    return pl.pallas_call(
        _conv_kernel,
        out_shape=jax.ShapeDtypeStruct((b, s, cw), BF16),
        grid=(b, s // TC),
        in_specs=[
            pl.BlockSpec((1, TC, cw), lambda bi, i: (bi, i, 0)),
            pl.BlockSpec((1, CONV_HALO, cw), lambda bi, i: (bi, jnp.maximum(i * per - 1, 0), 0)),
            _resident(w_dw.shape), _resident(b_dw.shape), _resident(ln_g.shape), _resident(ln_b.shape),
            _resident(g_out.shape),
        ],
        out_specs=pl.BlockSpec((1, TC, cw), lambda bi, i: (bi, i, 0)),
        scratch_shapes=[pltpu.VMEM((cw // LANES, TC + CONV_HALO, LANES), F32),
                        pltpu.VMEM((TC, cw), F32)],
        compiler_params=pltpu.CompilerParams(dimension_semantics=("arbitrary", "arbitrary"),
                                             vmem_limit_bytes=VMEM_LIMIT_BYTES),
        name="conformer_conv",
    )(glu, glu, w_dw, b_dw, ln_g, ln_b, g_out)


def _col_reduce(x, op):
    rows, cols = x.shape
    y = x.reshape(rows // 8, 8, cols)
    while y.shape[0] > 1:
        half = y.shape[0] // 2
        y = op(y[:half], y[half:])
    y = y[0]
    if op is jnp.add:
        return jnp.sum(y, axis=0, keepdims=True)
    return jnp.max(y, axis=0, keepdims=True)


def _key_to_float(key):
    bits = key ^ ((key >> 31) & 0x7FFFFFFF)
    return lax.bitcast_convert_type(bits, F32)


def _attn_kernel(safe_ref, shift_ref, iq_ref, q_ref, iwt_ref, ik_ref, k_ref, vt_ref, gout_ref, o_ref,
                 sc_ref, acc_ref, ot_ref, qp_ref, ml_ref, lsum_ref, sa_ref, sb_ref, rq_ref, za_ref, zb_ref, srt_ref,
                 *, top_k):
    i = pl.program_id(1)
    n_a = (i * TQ + TQ + TKA - 1) // TKA
    n_s = n_a * (TKA // TKS)

    lane_sq = lax.broadcasted_iota(jnp.int32, (TQ, LANES), 1)
    low_half = lane_sq < HEAD_DIM
    lane_row = lax.broadcasted_iota(jnp.int32, (1, TQ), 1)
    limit = i * TQ + (lane_row // CHUNK + 1) * CHUNK

    iqf = iq_ref[0].astype(F32)
    parts = []
    for hd in range(IDX_HEADS):
        blk = iqf[:, (hd // 2) * LANES:(hd // 2 + 1) * LANES]
        keep = low_half if hd % 2 == 0 else jnp.logical_not(low_half)
        parts.append(jnp.where(keep, blk, 0.0))
    rq_ref[...] = jnp.concatenate(parts, axis=0).astype(BF16)
    iwt = iwt_ref[0]

    def head_logits(j):
        r0 = pl.multiple_of(j * TKS, TKS)
        return lax.dot_general(ik_ref[0, pl.ds(r0, TKS), :], rq_ref[...], _NT,
                               preferred_element_type=F32)

    def score_tile(z_ref, j, has_inadmissible):
        r0 = pl.multiple_of(j * TKS, TKS)
        s = jnp.zeros((TKS, TQ), F32)
        for hd in range(IDX_HEADS):
            s = s + jnp.maximum(z_ref[:, hd * TQ:(hd + 1) * TQ], 0.0) * iwt[hd:hd + 1, :]
        if has_inadmissible:
            pos = r0 + lax.broadcasted_iota(jnp.int32, (TKS, TQ), 0)
            s = jnp.where(pos < limit, s, -jnp.inf)
        sc_ref[pl.ds(r0, TKS), :] = s
        rows = s.reshape(TKS // SUBLANES, SUBLANES, TQ)
        out = []
        for g0 in range(0, TKS // SUBLANES, SORT_GROUP):
            v = [rows[g0 + r] for r in range(SORT_GROUP)]
            for a, b in _SORT_PAIRS:
                v[a], v[b] = jnp.maximum(v[a], v[b]), jnp.minimum(v[a], v[b])
            out.extend(v)
        grp_rows = SORT_GROUP * SUBLANES
        for g in range(TKS // grp_rows):
            first = pl.multiple_of((j * (TKS // grp_rows) + g) * SORT_STRIDE, SUBLANES)
            grp = jnp.concatenate(out[g * SORT_GROUP:(g + 1) * SORT_GROUP], axis=0)
            for lb in range(TQ // LANES):
                srt_ref[lb, pl.ds(first, grp_rows), :] = grp[:, lb * LANES:(lb + 1) * LANES]

    za_ref[...] = head_logits(0)

    def score_tiles(t, last):
        if not last:
            zb_ref[...] = head_logits(2 * t + 1)
            score_tile(za_ref, 2 * t, False)
            za_ref[...] = head_logits(2 * t + 2)
            score_tile(zb_ref, 2 * t + 1, False)
            return
        second_live = (2 * t + 1) * TKS < TQ * (i + 1)

        @pl.when(second_live)
        def _both():
            zb_ref[...] = head_logits(2 * t + 1)
            score_tile(za_ref, 2 * t, True)
            score_tile(zb_ref, 2 * t + 1, True)

        @pl.when(jnp.logical_not(second_live))
        def _first_only():
            score_tile(za_ref, 2 * t, True)
            sc_ref[pl.ds(pl.multiple_of((2 * t + 1) * TKS, TKS), TKS), :] = jnp.full((TKS, TQ), -jnp.inf, F32)

    def score_tiles_body(t, carry):
        score_tiles(t, False)
        return carry

    lax.fori_loop(0, n_s // 2 - 1, score_tiles_body, 0)
    score_tiles(n_s // 2 - 1, True)

    n_grp = TKA // (SUBLANES * SORT_GROUP)
    last_tile_half = TQ * (i + 1) - TKA * (n_a - 1) <= TKA // 2

    def count_ge(t):
        def tile_count(j, groups):
            grp_rows = SORT_GROUP * SUBLANES
            v = jnp.stack([
                jnp.concatenate([
                    srt_ref[lb, pl.ds(pl.multiple_of((j * n_grp + g) * SORT_STRIDE, SUBLANES), grp_rows), :]
                    .reshape(SORT_GROUP, SUBLANES, LANES) for lb in range(TQ // LANES)], axis=-1)
                for g in range(groups)])
            cnt = jnp.where(v[:, SORT_GROUP - 1] >= t, 1, 0)
            window = [v[:, r] for r in range(SORT_GROUP - 1)]
            while window:
                half = len(window) // 2
                hit = window[half] >= t
                cnt = cnt + jnp.where(hit, half + 1, 0)
                window = [jnp.where(hit, window[half + 1 + r], window[r]) for r in range(half)]
            return cnt

        acc = lax.fori_loop(0, n_a - 1, lambda j, acc: acc + tile_count(j, n_grp),
                            jnp.zeros((n_grp, SUBLANES, TQ), jnp.int32))
        tail = lax.cond(last_tile_half,
                        lambda: jnp.concatenate([tile_count(n_a - 1, n_grp // 2),
                                                 jnp.zeros((n_grp - n_grp // 2, SUBLANES, TQ), jnp.int32)], axis=0),
                        lambda: tile_count(n_a - 1, n_grp))
        return jnp.sum((acc + tail).reshape(n_grp * SUBLANES, TQ), axis=0, keepdims=True)

    def bisect_bit(bit, carry):
        key, cnt_key = carry
        cand = key + lax.shift_left(jnp.int32(1), 31 - bit)
        cnt = count_ge(_key_to_float(cand))
        ok = cnt >= top_k
        return jnp.where(ok, cand, key), jnp.where(ok, cnt, cnt_key)

    key0 = jnp.full((1, TQ), INT_MIN, jnp.int32)
    key, cnt_key = lax.fori_loop(0, BISECT_CHECK_BIT, bisect_bit, (key0, jnp.zeros((1, TQ), jnp.int32)))
    unresolved = jnp.logical_and(cnt_key != top_k, key != INT_MIN)
    key, cnt_key = lax.cond(jnp.max(unresolved.astype(jnp.int32)) > 0,
                            lambda c: lax.fori_loop(BISECT_CHECK_BIT, 32, bisect_bit, c),
                            lambda c: c, (key, cnt_key))
    thr = jnp.where(key == INT_MIN, -FLT_MAX, _key_to_float(key))

    neg_shift = -shift_ref[...]

    tied = jnp.logical_and(cnt_key > top_k, key != INT_MIN)
    sub = lax.broadcasted_iota(jnp.int32, (SUBLANES, LANES), 0)
    tied_counts = jnp.sum(jnp.where(tied, lax.shift_left(jnp.int32(1), 8 * (lane_row // LANES)), 0))

    for blk_i in range(TQ // LANES):
        lanes = slice(blk_i * LANES, (blk_i + 1) * LANES)
        thr_b, shift_b, tied_b = thr[:, lanes], neg_shift[:, lanes], tied[:, lanes]
        any_tied = (lax.shift_right_logical(tied_counts, 8 * blk_i) & 0xFF) > 0

        @pl.when(jnp.logical_not(any_tied))
        def _mask(lanes=lanes, thr_b=thr_b, shift_b=shift_b):
            def bias_tile(j, carry):
                rows = pl.ds(pl.multiple_of(j * TKA, TKA), TKA)
                sc_ref[rows, lanes] = jnp.where(sc_ref[rows, lanes] >= thr_b, shift_b, NEG_BIG)
                return carry

            lax.fori_loop(0, n_a, bias_tile, 0)

        @pl.when(any_tied)
        def _mask_with_ties(lanes=lanes, thr_b=thr_b, shift_b=shift_b, tied_b=tied_b):
            def count_above(j, acc):
                rows = pl.ds(pl.multiple_of(j * TKA, TKA), TKA)
                ind = (sc_ref[rows, lanes] > thr_b).astype(jnp.int32)
                return acc + jnp.sum(ind.reshape(TKA // 32, 4, 8, LANES), axis=0)

            above = lax.fori_loop(0, n_a, count_above, jnp.zeros((4, 8, LANES), jnp.int32))
            above = jnp.sum(above.reshape(32, LANES), axis=0, keepdims=True)
            places = jnp.where(tied_b, top_k - above, jnp.int32(2 ** 30))

            def bias_tile(j, seen):
                rows = pl.ds(pl.multiple_of(j * TKA, TKA), TKA)
                tile = sc_ref[rows, lanes]
                out = []
                for g in range(TKA // SUBLANES):
                    s = tile[g * SUBLANES:(g + 1) * SUBLANES, :]
                    at_thr = s == thr_b
                    rank = at_thr.astype(jnp.int32)
                    for sh in (1, 2, 4):
                        rank = rank + jnp.where(sub >= sh, pltpu.roll(rank, sh, 0), 0)
                    keep = jnp.logical_or(s > thr_b, jnp.logical_and(at_thr, seen + rank <= places))
                    out.append(jnp.where(keep, shift_b, NEG_BIG))
                    seen = seen + rank[SUBLANES - 1:SUBLANES, :]
                sc_ref[rows, lanes] = jnp.concatenate(out, axis=0)
                return seen

            lax.fori_loop(0, n_a, bias_tile, jnp.zeros((1, LANES), jnp.int32))

    n_pair = ATTN_HEADS // 2
    qf = q_ref[0].astype(F32)
    for pr in range(n_pair):
        blk = qf[:, pr * LANES:(pr + 1) * LANES]
        qp_ref[pr] = jnp.concatenate([jnp.where(low_half, blk, 0.0),
                                      jnp.where(low_half, 0.0, blk)], axis=0).astype(BF16)
    acc_ref[...] = jnp.zeros_like(acc_ref)

    def qk_logits(j, pr):
        r0 = pl.multiple_of(j * TKA, TKA)
        kt = k_ref[0, pl.ds(r0, TKA), pr * LANES:(pr + 1) * LANES]
        return lax.dot_general(kt, qp_ref[pr], _NT, preferred_element_type=F32)

    def mask_bias(j):
        bias = sc_ref[pl.ds(pl.multiple_of(j * TKA, TKA), TKA), :]
        return jnp.concatenate([bias, bias], axis=1)

    def value_tile(j, pr):
        return vt_ref[0, j, pr * LANES:(pr + 1) * LANES, :]

    @pl.when(safe_ref[0, 0] != 0)
    def _fixed_shift():
        lsum_ref[...] = jnp.zeros_like(lsum_ref)
        for pr in range(n_pair):
            sa_ref[pr] = qk_logits(0, pr)

        def consume_tile(j, cur_ref, nxt_ref):
            bias2 = mask_bias(j)
            for pr in range(n_pair):
                if nxt_ref is not None:
                    nxt_ref[pr] = qk_logits(j + 1, pr)
                p = jnp.exp2(cur_ref[pr] + bias2)
                part = jnp.sum(p.reshape(TKA // 32, 4, 8, 2 * TQ), axis=0)
                lsum_ref[pr] += jnp.sum(part, axis=0)
                acc_ref[pr] += jnp.dot(value_tile(j, pr), p.astype(BF16), preferred_element_type=F32)

        def att_tiles4(t, carry):
            for u in range(0, 4, 2):
                consume_tile(4 * t + u, sa_ref, sb_ref)
                consume_tile(4 * t + u + 1, sb_ref, sa_ref)
            return carry

        looped = n_a - 1
        lax.fori_loop(0, looped // 4, att_tiles4, 0)
        after4 = (looped // 4) * 4

        @pl.when(looped - after4 >= 2)
        def _two_more():
            consume_tile(after4, sa_ref, sb_ref)
            consume_tile(after4 + 1, sb_ref, sa_ref)

        @pl.when(looped % 2 == 1)
        def _odd_then_last():
            consume_tile(n_a - 2, sa_ref, sb_ref)
            consume_tile(n_a - 1, sb_ref, None)

        @pl.when(looped % 2 == 0)
        def _last():
            consume_tile(n_a - 1, sa_ref, None)

        for pr in range(n_pair):
            ml_ref[n_pair + pr:n_pair + pr + 1, :] = jnp.sum(lsum_ref[pr], axis=0, keepdims=True)

    @pl.when(safe_ref[0, 0] == 0)
    def _running_max():
        ml_ref[0:n_pair, :] = jnp.full((n_pair, 2 * TQ), NEG_BIG, F32)
        ml_ref[n_pair:, :] = jnp.zeros((n_pair, 2 * TQ), F32)

        def att_tile(j, carry):
            bias2 = mask_bias(j)
            for pr in range(n_pair):
                s = qk_logits(j, pr) + bias2
                m_old = ml_ref[pr:pr + 1, :]
                m_new = jnp.maximum(m_old, _col_reduce(s, jnp.maximum))
                alpha = jnp.exp2(m_old - m_new)
                p = jnp.exp2(s - m_new)
                ml_ref[pr:pr + 1, :] = m_new
                ml_ref[n_pair + pr:n_pair + pr + 1, :] = (alpha * ml_ref[n_pair + pr:n_pair + pr + 1, :] +
                                                          _col_reduce(p, jnp.add))
                acc_ref[pr] = acc_ref[pr] * alpha + jnp.dot(value_tile(j, pr), p.astype(BF16),
                                                            preferred_element_type=F32)
            return carry

        lax.fori_loop(0, n_a, att_tile, 0)

    for pr in range(n_pair):
        l = ml_ref[n_pair + pr:n_pair + pr + 1, :]
        ot_ref[pr * LANES:pr * LANES + HEAD_DIM, :] = acc_ref[pr, 0:HEAD_DIM, 0:TQ] / l[:, 0:TQ]
        ot_ref[pr * LANES + HEAD_DIM:(pr + 1) * LANES, :] = acc_ref[pr, HEAD_DIM:LANES, TQ:2 * TQ] / l[:, TQ:2 * TQ]

    attn = ot_ref[...].T
    ms = jnp.mean(attn * attn, axis=-1, keepdims=True)
    o_ref[0] = (attn * lax.rsqrt(ms + EPS) * gout_ref[...]).astype(o_ref.dtype)


def _attn_call(safe, shift, iq, q, iwt, ik, k, vt, g_out):
    b, s, aw = q.shape
    top_k = min(TOPK_MAX, s // 4)
    qtile = lambda bi, i: (bi, i, 0)
    per_batch3 = lambda bi, i: (bi, 0, 0)
    return pl.pallas_call(
        functools.partial(_attn_kernel, top_k=top_k),
        out_shape=jax.ShapeDtypeStruct((b, s, aw), BF16),
        grid=(b, s // TQ),
        in_specs=[
            pl.BlockSpec(memory_space=pltpu.SMEM),
            _resident(shift.shape),
            pl.BlockSpec((1, TQ, aw), qtile),
            pl.BlockSpec((1, TQ, aw), qtile),
            pl.BlockSpec((1, IDX_HEADS, TQ), lambda bi, i: (bi, 0, i)),
            pl.BlockSpec((1, s, LANES), per_batch3, pipeline_mode=pl.Buffered(1)),
            pl.BlockSpec((1, s, aw), per_batch3, pipeline_mode=pl.Buffered(1)),
            pl.BlockSpec((1, s // TKA, aw, TKA), lambda bi, i: (bi, 0, 0, 0), pipeline_mode=pl.Buffered(1)),
            _resident(g_out.shape),
        ],
        out_specs=pl.BlockSpec((1, TQ, aw), qtile),
        scratch_shapes=[pltpu.VMEM((s, TQ), F32),
                        pltpu.VMEM((ATTN_HEADS // 2, LANES, 2 * TQ), F32),
                        pltpu.VMEM((aw, TQ), F32),
                        pltpu.VMEM((ATTN_HEADS // 2, 2 * TQ, LANES), BF16),
                        pltpu.VMEM((ATTN_HEADS, 2 * TQ), F32),
                        pltpu.VMEM((ATTN_HEADS // 2, 8, 2 * TQ), F32),
                        pltpu.VMEM((ATTN_HEADS // 2, TKA, 2 * TQ), F32),
                        pltpu.VMEM((ATTN_HEADS // 2, TKA, 2 * TQ), F32),
                        pltpu.VMEM((IDX_HEADS * TQ, LANES), BF16),
                        pltpu.VMEM((TKS, IDX_HEADS * TQ), F32),
                        pltpu.VMEM((TKS, IDX_HEADS * TQ), F32),
                        pltpu.VMEM((TQ // LANES, s // (SORT_GROUP * SUBLANES) * SORT_STRIDE, LANES),
                                   F32)],
        compiler_params=pltpu.CompilerParams(dimension_semantics=("arbitrary", "arbitrary"),
                                             vmem_limit_bytes=VMEM_LIMIT_BYTES),
        name="dsa_attention",
    )(safe, shift, iq, q, iwt, ik, k, vt, g_out)


def _ffn_kernel(x_ref, a_ref, c_ref, woa_ref, woc_ref, gt1_ref, g2_ref, sc2_ref, sh2_ref, gt2_ref,
                w1_ref, w2_ref, o_ref, *, ff_chunk):
    y = (jnp.dot(a_ref[0], woa_ref[...], preferred_element_type=F32) +
         jnp.dot(c_ref[0], woc_ref[...], preferred_element_type=F32))
    x1 = x_ref[0] + gt1_ref[0] * y
    ms = jnp.mean(x1 * x1, axis=-1, keepdims=True)
    h2 = ((x1 * lax.rsqrt(ms + EPS) * g2_ref[...]) * (1.0 + sc2_ref[0]) + sh2_ref[0]).astype(BF16)
    f = jnp.zeros(x1.shape, F32)
    for cidx in range(w1_ref.shape[1] // ff_chunk):
        a = jnp.dot(h2, w1_ref[:, cidx * ff_chunk:(cidx + 1) * ff_chunk], preferred_element_type=F32)
        a = jnp.maximum(a, 0.0)
        f = f + jnp.dot((a * a).astype(BF16), w2_ref[cidx * ff_chunk:(cidx + 1) * ff_chunk, :],
                        preferred_element_type=F32)
    o_ref[0] = x1 + gt2_ref[0] * f


def _ffn_call(x, attn_n, conv_n, woa, woc, gt1, g2, sc2, sh2, gt2, w1, w2):
    b, s, d = x.shape
    tile = lambda bi, i: (bi, i, 0)
    row = lambda bi, i: (bi, 0, 0)
    hw = attn_n.shape[-1]
    return pl.pallas_call(
        functools.partial(_ffn_kernel, ff_chunk=1024),
        out_shape=jax.ShapeDtypeStruct((b, s, d), F32),
        grid=(b, s // TM),
        in_specs=[
            pl.BlockSpec((1, TM, d), tile),
            pl.BlockSpec((1, TM, hw), tile),
            pl.BlockSpec((1, TM, hw), tile),
            _resident(woa.shape), _resident(woc.shape),
            pl.BlockSpec((1, 1, d), row),
            _resident(g2.shape),
            pl.BlockSpec((1, 1, d), row),
            pl.BlockSpec((1, 1, d), row),
            pl.BlockSpec((1, 1, d), row),
            _resident(w1.shape), _resident(w2.shape),
        ],
        out_specs=pl.BlockSpec((1, TM, d), tile),
        compiler_params=pltpu.CompilerParams(dimension_semantics=("arbitrary", "arbitrary"),
                                             vmem_limit_bytes=VMEM_LIMIT_BYTES),
        name="outproj_ffn",
    )(x, attn_n, conv_n, woa, woc, gt1, g2, sc2, sh2, gt2, w1, w2)


def _rope_lane_tables(seq_len):
    half = ROT_DIM // 2
    pos = np.arange(seq_len, dtype=np.float32)
    inv_freq = np.float32(ROPE_THETA) ** (-np.arange(0, ROT_DIM, 2, dtype=np.float32) / np.float32(ROT_DIM))
    ang = pos[:, None] * inv_freq[None, :].astype(np.float32)
    cos, sin = jnp.asarray(np.cos(ang), F32), jnp.asarray(np.sin(ang), F32)
    ones = jnp.ones((seq_len, HEAD_DIM - ROT_DIM), F32)
    zeros_h = jnp.zeros((seq_len, half), F32)
    zeros_r = jnp.zeros((seq_len, HEAD_DIM - ROT_DIM), F32)
    cos_h = jnp.concatenate([cos, cos, ones], axis=1)
    sina_h = jnp.concatenate([-sin, zeros_h, zeros_r], axis=1)
    sinb_h = jnp.concatenate([zeros_h, sin, zeros_r], axis=1)
    two = lambda t: jnp.concatenate([t, t], axis=1)
    return two(cos_h), two(sina_h), two(sinb_h)


def kernel(x, c, w_ada, b_ada, g_norm1, w_in, g_q, g_k, w_dw, b_dw, g_conv_ln, b_conv_ln,
           g_out_attn, g_out_conv, w_out, g_norm2, w_ff1, w_ff2):
    b, s, d = x.shape
    depth = w_ada.shape[0]
    aw = ATTN_WIDTH
    cos_t, sina_t, sinb_t = _rope_lane_tables(s)
    head_id = jnp.arange(aw) // HEAD_DIM
    gmat = (head_id[:, None] == head_id[None, :]).astype(BF16) * (1.0 / HEAD_DIM)
    c_pad = jnp.zeros((SUBLANES, d), F32).at[:b].set(c)
    o = [0, aw, 2 * aw, 3 * aw, 4 * aw, 4 * aw + IDX_DIM, 4 * aw + IDX_DIM + IDX_HEADS]
    cw = d - aw

    for l in range(depth):
        mod = _mod_call(c_pad, w_ada[l], b_ada[l][None, :])[:b]
        sh1, sc1, gt1, sh2, sc2, gt2 = [m[:, None, :] for m in jnp.split(mod, N_MOD, axis=-1)]

        wl = w_in[l].astype(BF16)
        wq, wk, wv, wiq = (wl[:, o[j]:o[j + 1]] for j in range(4))
        wik = wl[:, o[4]:o[5]]
        wik2 = jnp.concatenate([wik, wik], axis=1)
        wiwt = wl[:, o[5]:o[6]].T
        wca = wl[:, o[6]:o[6] + cw]
        wcg = wl[:, o[6] + cw:o[6] + 2 * cw]
        tile8 = lambda g: jnp.tile(g, ATTN_HEADS)[None, :]

        q, k, vt, iq, ik, iwt, glu = _inproj_call(
            x, g_norm1[l][None, :], sc1, sh1, wq, wk, wv.T, wiq, wik2, wiwt, wca, wcg,
            tile8(g_q[l]), tile8(g_k[l]), gmat, cos_t, sina_t, sinb_t)

        conv_n = _conv_call(glu, w_dw[l], b_dw[l][None, :], g_conv_ln[l][None, :], b_conv_ln[l][None, :],
                            g_out_conv[l][None, :])
        bound = (1.02 * LOG2E * HEAD_DIM ** 0.5) * jnp.max(jnp.abs(g_q[l])) * jnp.max(jnp.abs(g_k[l]))
        safe = bound <= MAX_FIXED_SHIFT
        shift = jnp.full((1, TQ), jnp.where(safe, bound, 0.0), F32)
        attn_n = _attn_call(safe.astype(jnp.int32).reshape(1, 1), shift, iq, q, iwt, ik, k, vt,
                            g_out_attn[l][None, :])

        wo = w_out[l].astype(BF16)
        x = _ffn_call(x, attn_n, conv_n, wo[:aw], wo[aw:], gt1, g_norm2[l][None, :], sc2, sh2, gt2,
                      w_ff1[l].astype(BF16), w_ff2[l].astype(BF16))
    return x
```
